```python
import math
import jax, jax.numpy as jnp
from jax import lax
import numpy as np

D_MODEL = 2048
BATCH = 1
SEQ = 8192
DEPTH = 4
DEC_BATCH = 8
DEC_SEQ = 16
PAST_LEN = 1024

F32 = jnp.float32
CHUNK = 64
EPS = 1e-6
ROPE_THETA = 10000.0

SSD_WIDTH = D_MODEL
SSD_HEAD_DIM = 64
SSD_HEADS = SSD_WIDTH // SSD_HEAD_DIM
SSD_GROUPS = 4
SSD_STATE = 128
SSD_CONV = 4
SSD_XBC = SSD_WIDTH + 2 * SSD_GROUPS * SSD_STATE
DT_MIN = 0.001
DT_MAX = 0.1

ATT_HEADS = 8
ATT_KV_HEADS = 2
ATT_HEAD_DIM = 128
ATT_WIDTH = ATT_HEADS * ATT_HEAD_DIM
IDX_HEADS = 8
IDX_DIM = 64
TOPK_MAX = 256
Q_BLOCK = 128

POOL_WIDTH = D_MODEL // 2
POOL_WINDOWS = (2, 4, 8, 16)
POOL_GROUPS = 4
POOL_GROUP_DIM = POOL_WIDTH // POOL_GROUPS
POOL_HIST = max(POOL_WINDOWS) - 1

N_BRANCH = 3
IN_SPLIT_SIZES = (SSD_WIDTH, SSD_XBC, SSD_HEADS,
                  ATT_WIDTH, ATT_KV_HEADS * ATT_HEAD_DIM, ATT_KV_HEADS * ATT_HEAD_DIM,
                  IDX_HEADS * IDX_DIM, IDX_DIM, IDX_HEADS,
                  POOL_WIDTH, N_BRANCH * D_MODEL)
IN_COLS = sum(IN_SPLIT_SIZES)

N_EXPERT_GROUPS = 4
EXPERTS_PER_GROUP = 8
N_EXPERTS = N_EXPERT_GROUPS * EXPERTS_PER_GROUP
TOP_K_INNER = 2
EXPERT_FF = 512
MOE_BLOCK = 64

kernel_name = 'hybrid_streaming_ssd_dsa_pool_hmoe_step'


def in_split_points():
    pts, acc = [], 0
    for s in IN_SPLIT_SIZES[:-1]:
        acc += s
        pts.append(acc)
    return pts


def rms_norm(x, g):
    xf = x.astype(F32)
    y = xf * lax.rsqrt(jnp.mean(xf * xf, axis=-1, keepdims=True) + EPS)
    return (y * g.astype(F32)).astype(x.dtype)


def rope(x, pos):
    half = x.shape[-1] // 2
    inv = ROPE_THETA ** (-jnp.arange(half, dtype=F32) / half)
    ang = pos.astype(F32)[:, None] * inv[None, :]
    cos = jnp.cos(ang)[:, None, :]
    sin = jnp.sin(ang)[:, None, :]
    x1 = x[..., :half].astype(F32)
    x2 = x[..., half:].astype(F32)
    return jnp.concatenate([x1 * cos - x2 * sin, x2 * cos + x1 * sin], axis=-1).astype(x.dtype)


def ssd_scan(xs, dt, a_log, bm, cm, s0):
    bsz, L, H, P = xs.shape
    G, N = bm.shape[2], bm.shape[3]
    hg = H // G
    Q = min(CHUNK, L)
    nc = L // Q
    a = dt * (-jnp.exp(a_log.astype(F32)))

    def to_chunks(t):
        return jnp.moveaxis(t.reshape((bsz, nc, Q) + t.shape[2:]), 1, 0)

    seq_in = (to_chunks(xs.astype(F32).reshape(bsz, L, G, hg, P)),
              to_chunks(dt.reshape(bsz, L, G, hg)),
              to_chunks(a.reshape(bsz, L, G, hg)),
              to_chunks(bm.astype(F32)),
              to_chunks(cm.astype(F32)))
    causal = jnp.tril(jnp.ones((Q, Q), dtype=bool))

    def step(S, inp):
        xc, dtc, ac, bc, cc = inp
        cum = jnp.cumsum(ac, axis=1)
        seg = cum[:, :, None] - cum[:, None, :]
        decay = jnp.exp(jnp.where(causal[None, :, :, None, None], seg, -jnp.inf))
        cb = jnp.einsum('btgn,bsgn->btsg', cc, bc)
        wts = cb[..., None] * decay * dtc[:, None]
        y_intra = jnp.einsum('btsgh,bsghp->btghp', wts, xc)
        y_inter = jnp.einsum('btgn,bghpn->btghp', cc, S) * jnp.exp(cum)[..., None]
        last = cum[:, -1]
        w_state = jnp.exp(last[:, None] - cum) * dtc
        S_new = S * jnp.exp(last)[..., None, None] + jnp.einsum('bsgh,bsghp,bsgn->bghpn', w_state, xc, bc)
        return S_new, y_intra + y_inter

    S_fin, ys = lax.scan(step, s0.astype(F32).reshape(bsz, G, hg, P, N), seq_in)
    y = jnp.moveaxis(ys, 0, 1).reshape(bsz, L, H, P)
    return y, S_fin.reshape(bsz, H, P, N)


def ssd_mixer(z, xbc, dt_raw, conv_hist, ssm0, conv_w, conv_b, dt_bias, a_log, d_skip, ssd_norm):
    bsz, T, _ = z.shape
    full = jnp.concatenate([conv_hist.astype(xbc.dtype), xbc], axis=1)
    conv = conv_b + sum(full[:, j:j + T] * conv_w[j] for j in range(SSD_CONV))
    conv_new = full[:, -(SSD_CONV - 1):]
    act = jax.nn.silu(conv)
    gn = SSD_GROUPS * SSD_STATE
    xs, bm, cm = jnp.split(act, [SSD_WIDTH, SSD_WIDTH + gn], axis=-1)
    xs = xs.reshape(bsz, T, SSD_HEADS, SSD_HEAD_DIM)
    dt = jax.nn.softplus(dt_raw.astype(F32) + dt_bias.astype(F32))
    y, s_new = ssd_scan(xs, dt, a_log,
                        bm.reshape(bsz, T, SSD_GROUPS, SSD_STATE),
                        cm.reshape(bsz, T, SSD_GROUPS, SSD_STATE), ssm0)
    y = y + d_skip.astype(F32)[:, None] * xs.astype(F32)
    y = y.reshape(bsz, T, SSD_WIDTH) * jax.nn.silu(z.astype(F32))
    return rms_norm(y, ssd_norm).astype(z.dtype), conv_new, s_new


def dsa_attention(q, iq, iw, k, v, ik, q_pos):
    bsz, T = q.shape[0], q.shape[1]
    S = k.shape[1]
    topk = min(TOPK_MAX, S // 4)
    qb = min(Q_BLOCK, T)
    nb = T // qb
    grp = ATT_HEADS // ATT_KV_HEADS
    key_pos = jnp.arange(S)
    bidx = jnp.arange(bsz)[:, None, None]
    ikf = ik.astype(F32)

    def block(args):
        qc, iqc, iwc, pc = args
        limit = (pc // CHUNK + 1) * CHUNK
        adm = key_pos[None, :] < limit[:, None]
        rel = jax.nn.relu(jnp.einsum('bthd,bsd->bths', iqc.astype(F32), ikf) * IDX_DIM ** -0.5)
        score = jnp.einsum('bths,bth->bts', rel, iwc.astype(F32) * IDX_HEADS ** -0.5)
        score = jnp.where(adm[None], score, -jnp.inf)
        _, idx = lax.top_k(score, topk)
        sel_ok = idx < limit[None, :, None]
        ksel = k[bidx, idx].astype(F32)
        vsel = v[bidx, idx].astype(F32)
        qg = qc.astype(F32).reshape(bsz, qb, ATT_KV_HEADS, grp, ATT_HEAD_DIM)
        logits = jnp.einsum('btkgd,btskd->btkgs', qg, ksel) * ATT_HEAD_DIM ** -0.5
        logits = jnp.where(sel_ok[:, :, None, None, :], logits, -jnp.inf)
        prob = jax.nn.softmax(logits, axis=-1)
        o = jnp.einsum('btkgs,btskd->btkgd', prob, vsel)
        return o.reshape(bsz, qb, ATT_WIDTH).astype(q.dtype)

    def to_blocks(t):
        return jnp.moveaxis(t.reshape((bsz, nb, qb) + t.shape[2:]), 1, 0)

    outs = lax.map(block, (to_blocks(q), to_blocks(iq), to_blocks(iw), q_pos.reshape(nb, qb)))
    return jnp.moveaxis(outs, 0, 1).reshape(bsz, T, ATT_WIDTH)


def pool_mixer(u, hist, pos, pool_lin, pool_scale):
    bsz, T, C = u.shape
    full = jnp.concatenate([hist.astype(u.dtype), u], axis=1)
    fullf = full.astype(F32)
    cs = jnp.concatenate([jnp.zeros((bsz, 1, C), F32), jnp.cumsum(fullf, axis=1)], axis=1)
    end = cs[:, POOL_HIST + 1:]
    parts = []
    for gi, w in enumerate(POOL_WINDOWS):
        lo, hi = gi * POOL_GROUP_DIM, (gi + 1) * POOL_GROUP_DIM
        start = cs[:, POOL_HIST + 1 - w: POOL_HIST + 1 - w + T, lo:hi]
        cnt = jnp.minimum(pos + 1, w).astype(F32)[None, :, None]
        parts.append((end[..., lo:hi] - start) / cnt)
    pooled = jnp.concatenate(parts, axis=-1) - u.astype(F32)
    pooled = pooled.reshape(bsz, T, POOL_GROUPS, POOL_GROUP_DIM)
    y = jnp.einsum('btgc,gcd->btgd', pooled, pool_lin.astype(F32)).reshape(bsz, T, C)
    y = y * pool_scale.astype(F32)
    return y.astype(u.dtype), full[:, -POOL_HIST:]


def hier_moe(h, w_rg, b_rg, w_re, b_re, w_gate, w_up, w_down):
    bsz, T, D = h.shape
    n = bsz * T
    t = h.reshape(n, D)
    lg = jnp.dot(t, w_rg).astype(F32) + b_rg.astype(F32)
    grp = jnp.argmax(lg, axis=-1)
    g_w = jnp.take_along_axis(jax.nn.softmax(lg, axis=-1), grp[:, None], axis=-1)
    le = (jnp.dot(t, w_re).astype(F32) + b_re.astype(F32)).reshape(n, N_EXPERT_GROUPS, EXPERTS_PER_GROUP)
    le = jnp.take_along_axis(le, grp[:, None, None], axis=1)[:, 0]
    top_l, top_i = lax.top_k(le, TOP_K_INNER)
    gate = (g_w * jax.nn.softmax(top_l, axis=-1)).reshape(-1)
    eid = (grp[:, None] * EXPERTS_PER_GROUP + top_i).reshape(-1).astype(jnp.int32)
    m = eid.shape[0]
    order = jnp.argsort(eid)
    se = eid[order]
    sizes = jnp.bincount(eid, length=N_EXPERTS).astype(jnp.int32)
    padded = (sizes + MOE_BLOCK - 1) // MOE_BLOCK * MOE_BLOCK
    pad_end = jnp.cumsum(padded)
    pad_start = pad_end - padded
    start = jnp.cumsum(sizes) - sizes
    dest = pad_start[se] + jnp.arange(m, dtype=jnp.int32) - start[se]
    n_blocks = -(-m // MOE_BLOCK) + N_EXPERTS
    tok = order // TOP_K_INNER
    buf = jnp.zeros((n_blocks * MOE_BLOCK, D), t.dtype).at[dest].set(t[tok])
    blk_e = jnp.minimum(jnp.searchsorted(pad_end, jnp.arange(n_blocks, dtype=jnp.int32) * MOE_BLOCK, side='right'),
                        N_EXPERTS - 1)

    def expert_block(args):
        xb, e = args
        return jnp.dot(jax.nn.silu(jnp.dot(xb, w_gate[e])) * jnp.dot(xb, w_up[e]), w_down[e])

    out = lax.map(expert_block, (buf.reshape(n_blocks, MOE_BLOCK, D), blk_e)).reshape(-1, D)
    contrib = out[dest] * gate[order][:, None].astype(out.dtype)
    return jax.ops.segment_sum(contrib, tok, num_segments=n).reshape(bsz, T, D)


def trunk_layer(x, c, pos, conv_hist, ssm0, k_past, v_past, ik_past, pool_hist, wl):
    (w_mod, b_mod, norm_mix, w_in, conv_w, conv_b, dt_bias, a_log, d_skip, ssd_norm,
     w_br_a, w_br_b, w_br_c, w_out, pool_lin, pool_scale, norm_ffn,
     w_rg, b_rg, w_re, b_re, w_gate, w_up, w_down) = wl
    bsz, T, _ = x.shape
    mod = jnp.dot(jax.nn.silu(c), w_mod) + b_mod
    sh1, sc1, g1, sh2, sc2, g2 = jnp.split(mod[:, None, :], 6, axis=-1)
    h = rms_norm(x, norm_mix) * (1 + sc1) + sh1
    proj = jnp.dot(h, w_in)
    z, xbc, dt_raw, q, k, v, iq, ik, iw, u, gate_logits = jnp.split(proj, in_split_points(), axis=-1)

    ya, conv_new, ssm_new = ssd_mixer(z, xbc, dt_raw, conv_hist, ssm0, conv_w, conv_b,
                                      dt_bias, a_log, d_skip, ssd_norm)

    q = rope(q.reshape(bsz, T, ATT_HEADS, ATT_HEAD_DIM), pos)
    k = rope(k.reshape(bsz, T, ATT_KV_HEADS, ATT_HEAD_DIM), pos)
    v = v.reshape(bsz, T, ATT_KV_HEADS, ATT_HEAD_DIM)
    iq = rope(iq.reshape(bsz, T, IDX_HEADS, IDX_DIM), pos)
    ik = rope(ik.reshape(bsz, T, 1, IDX_DIM), pos)[:, :, 0]
    k_all = jnp.concatenate([k_past.astype(k.dtype), k], axis=1)
    v_all = jnp.concatenate([v_past.astype(v.dtype), v], axis=1)
    ik_all = jnp.concatenate([ik_past.astype(ik.dtype), ik], axis=1)
    yb = dsa_attention(q, iq, iw, k_all, v_all, ik_all, pos)

    yc, pool_new = pool_mixer(u, pool_hist, pos, pool_lin, pool_scale)

    ga, gb, gc = jnp.split(jax.nn.sigmoid(gate_logits.astype(F32)), N_BRANCH, axis=-1)
    merged = ga * jnp.dot(ya, w_br_a) + gb * jnp.dot(yb, w_br_b) + gc * jnp.dot(yc, w_br_c)
    x = x + (g1 * jnp.dot(merged.astype(x.dtype), w_out)).astype(x.dtype)

    h2 = rms_norm(x, norm_ffn) * (1 + sc2) + sh2
    x = x + (g2 * hier_moe(h2, w_rg, b_rg, w_re, b_re, w_gate, w_up, w_down)).astype(x.dtype)
    return x, (k, v, ik, conv_new, ssm_new, pool_new)


def setup_inputs(seed: int = 0) -> dict:
    key = jax.random.key(seed)
    keys = iter(jax.random.split(key, 64))

    def nrm(shape, scale):
        return jax.random.normal(next(keys), shape, F32) * scale

    def gain(shape):
        return 1.0 + nrm(shape, 0.1)

    D = D_MODEL
    u_dt = jax.random.uniform(next(keys), (DEPTH, SSD_HEADS), F32)
    dt0 = jnp.exp(u_dt * (math.log(DT_MAX) - math.log(DT_MIN)) + math.log(DT_MIN))
    dt_bias = dt0 + jnp.log(-jnp.expm1(-dt0))
    a_log = jnp.log(jax.random.uniform(next(keys), (DEPTH, SSD_HEADS), F32, 1.0, 16.0))
    return {
        'x_prompt': nrm((BATCH, SEQ, D), 1.0),
        'x_sample': nrm((DEC_BATCH, DEC_SEQ, D), 1.0),
        'cache_k': nrm((DEPTH, DEC_BATCH, PAST_LEN, ATT_KV_HEADS, ATT_HEAD_DIM), 1.0),
        'cache_v': nrm((DEPTH, DEC_BATCH, PAST_LEN, ATT_KV_HEADS, ATT_HEAD_DIM), 1.0),
        'cache_idx_k': nrm((DEPTH, DEC_BATCH, PAST_LEN, IDX_DIM), 1.0),
        'state_conv': nrm((DEPTH, DEC_BATCH, SSD_CONV - 1, SSD_XBC), 1.0),
        'state_ssm': nrm((DEPTH, DEC_BATCH, SSD_HEADS, SSD_HEAD_DIM, SSD_STATE), 0.1),
        'state_pool': nrm((DEPTH, DEC_BATCH, POOL_HIST, POOL_WIDTH), 1.0),
        'c_prompt': nrm((BATCH, D), 1.0),
        'c_sample': nrm((DEC_BATCH, D), 1.0),
        'w_mod': nrm((DEPTH, D, 6 * D), 0.5 * D ** -0.5),
        'b_mod': nrm((DEPTH, 6 * D), 0.01),
        'norm_mix': gain((DEPTH, D)),
        'w_in': nrm((DEPTH, D, IN_COLS), D ** -0.5),
        'conv_w': nrm((DEPTH, SSD_CONV, SSD_XBC), 0.5),
        'conv_b': nrm((DEPTH, SSD_XBC), 0.01),
        'dt_bias': dt_bias,
        'a_log': a_log,
        'd_skip': gain((DEPTH, SSD_HEADS)),
        'ssd_norm': gain((DEPTH, SSD_WIDTH)),
        'w_br_a': nrm((DEPTH, SSD_WIDTH, D), SSD_WIDTH ** -0.5),
        'w_br_b': nrm((DEPTH, ATT_WIDTH, D), ATT_WIDTH ** -0.5),
        'w_br_c': nrm((DEPTH, POOL_WIDTH, D), POOL_WIDTH ** -0.5),
        'w_out': nrm((DEPTH, D, D), D ** -0.5),
        'pool_lin': nrm((DEPTH, POOL_GROUPS, POOL_GROUP_DIM, POOL_GROUP_DIM), POOL_GROUP_DIM ** -0.5),
        'pool_scale': gain((DEPTH, POOL_WIDTH)),
        'norm_ffn': gain((DEPTH, D)),
        'w_router_group': nrm((DEPTH, D, N_EXPERT_GROUPS), D ** -0.5),
        'b_router_group': nrm((DEPTH, N_EXPERT_GROUPS), 0.01),
        'w_router_expert': nrm((DEPTH, D, N_EXPERTS), D ** -0.5),
        'b_router_expert': nrm((DEPTH, N_EXPERTS), 0.01),
        'w_gate': nrm((DEPTH, N_EXPERTS, D, EXPERT_FF), D ** -0.5),
        'w_up': nrm((DEPTH, N_EXPERTS, D, EXPERT_FF), D ** -0.5),
        'w_down': nrm((DEPTH, N_EXPERTS, EXPERT_FF, D), EXPERT_FF ** -0.5),
        'final_norm': gain((D,)),
    }


def reference(x_prompt, x_sample, cache_k, cache_v, cache_idx_k, state_conv, state_ssm, state_pool,
              c_prompt, c_sample, w_mod, b_mod, norm_mix, w_in, conv_w, conv_b, dt_bias, a_log, d_skip,
              ssd_norm, w_br_a, w_br_b, w_br_c, w_out, pool_lin, pool_scale, norm_ffn,
              w_router_group, b_router_group, w_router_expert, b_router_expert,
              w_gate, w_up, w_down, final_norm):
    bp, tp = x_prompt.shape[0], x_prompt.shape[1]
    ts = x_sample.shape[1]
    dt = x_prompt.dtype
    pos_p = jnp.arange(tp, dtype=jnp.int32)
    pos_s = cache_k.shape[2] + jnp.arange(ts, dtype=jnp.int32)
    conv0 = jnp.zeros((bp, SSD_CONV - 1, SSD_XBC), dt)
    ssm0 = jnp.zeros((bp, SSD_HEADS, SSD_HEAD_DIM, SSD_STATE), F32)
    k0 = jnp.zeros((bp, 0, ATT_KV_HEADS, ATT_HEAD_DIM), dt)
    ik0 = jnp.zeros((bp, 0, IDX_DIM), dt)
    pool0 = jnp.zeros((bp, POOL_HIST, POOL_WIDTH), dt)

    layer_weights = (w_mod, b_mod, norm_mix, w_in, conv_w, conv_b, dt_bias, a_log, d_skip, ssd_norm,
                     w_br_a, w_br_b, w_br_c, w_out, pool_lin, pool_scale, norm_ffn,
                     w_router_group, b_router_group, w_router_expert, b_router_expert,
                     w_gate, w_up, w_down)
    xp, xs = x_prompt, x_sample
    st_p, st_s = [], []
    for l in range(DEPTH):
        wl = tuple(a[l] for a in layer_weights)
        xp, sp = trunk_layer(xp, c_prompt, pos_p, conv0, ssm0, k0, k0, ik0, pool0, wl)
        xs, ss = trunk_layer(xs, c_sample, pos_s, state_conv[l], state_ssm[l], cache_k[l], cache_v[l],
                             cache_idx_k[l], state_pool[l], wl)
        st_p.append(sp)
        st_s.append(ss)
    y_prompt = rms_norm(xp, final_norm)
    y_sample = rms_norm(xs, final_norm)
    new_k_prompt = jnp.stack([s[0] for s in st_p])
    new_v_prompt = jnp.stack([s[1] for s in st_p])
    new_idx_k_prompt = jnp.stack([s[2] for s in st_p])
    new_conv_prompt = jnp.stack([s[3] for s in st_p])
    new_ssm_prompt = jnp.stack([s[4] for s in st_p])
    new_pool_prompt = jnp.stack([s[5] for s in st_p])
    new_k_sample = jnp.stack([s[0] for s in st_s])
    new_v_sample = jnp.stack([s[1] for s in st_s])
    new_idx_k_sample = jnp.stack([s[2] for s in st_s])
    new_conv_sample = jnp.stack([s[3] for s in st_s])
    new_ssm_sample = jnp.stack([s[4] for s in st_s])
    new_pool_sample = jnp.stack([s[5] for s in st_s])
    return (y_prompt, y_sample,
            new_k_prompt, new_v_prompt, new_idx_k_prompt, new_conv_prompt, new_ssm_prompt, new_pool_prompt,
            new_k_sample, new_v_sample, new_idx_k_sample, new_conv_sample, new_ssm_sample, new_pool_sample)
```

```python
import functools
import math

import jax
import jax.numpy as jnp
from jax import lax
from jax.experimental import pallas as pl
from jax.experimental.pallas import tpu as pltpu

F32 = jnp.float32
BF16 = jnp.bfloat16
I32 = jnp.int32

LANES = 128
CHUNK = 64
EPS = 1e-6
ROPE_THETA = 10000.0
SSD_HEAD_DIM = 64
SSD_GROUPS = 4
SSD_STATE = 128
SSD_CONV = 4
ATT_HEADS = 8
ATT_KV_HEADS = 2
ATT_HEAD_DIM = 128
IDX_HEADS = 8
IDX_DIM = 64
TOPK_MAX = 256
POOL_WINDOWS = (2, 4, 8, 16)
POOL_HIST = 15
N_EXPERT_GROUPS = 4
EXPERTS_PER_GROUP = 8
N_EXPERTS = 32
NEG_BIG = -1e30
VMEM_LIMIT = 56 * 1024 * 1024

SM_DT = 0
SM_IW = 32
SM_IK = 64


def _cparams(sem):
    return pltpu.CompilerParams(dimension_semantics=sem, vmem_limit_bytes=VMEM_LIMIT)


def _dot(a, b):
    return jnp.dot(a, b, preferred_element_type=F32)


def _dot_nt(a, b):
    return lax.dot_general(a, b, (((1,), (1,)), ((), ())), preferred_element_type=F32)


def _sigmoid(x):
    return 1.0 / (1.0 + jnp.exp(-x))


def _silu(x):
    return x * _sigmoid(x)


def _mod_kernel(c_ref, w_ref, b_ref, o_ref):
    s = _silu(c_ref[...]).astype(BF16)
    o_ref[0] = _dot(s, w_ref[0].astype(BF16)) + b_ref[0]


def _mod_all(c_all, w_mod, b_mod, tn=1024):
    depth, d, n = w_mod.shape
    r = c_all.shape[0]
    return pl.pallas_call(
        _mod_kernel,
        grid=(depth, n // tn),
        in_specs=[pl.BlockSpec((r, d), lambda l, j: (0, 0)),
                  pl.BlockSpec((1, d, tn), lambda l, j: (l, 0, j)),
                  pl.BlockSpec((1, 1, tn), lambda l, j: (l, 0, j))],
        out_specs=pl.BlockSpec((1, r, tn), lambda l, j: (l, 0, j)),
        out_shape=jax.ShapeDtypeStruct((depth, r, n), F32),
        compiler_params=_cparams(("arbitrary", "arbitrary")),
        name="mod",
    )(c_all, w_mod, b_mod.reshape(depth, 1, n))


def _norm_mod_kernel(x_ref, g_ref, sc_ref, sh_ref, o_ref):
    x = x_ref[...]
    y = x * lax.rsqrt(jnp.mean(x * x, axis=-1, keepdims=True) + EPS) * g_ref[...]
    o_ref[...] = (y * (1.0 + sc_ref[...]) + sh_ref[...]).astype(o_ref.dtype)


def _row_spec(rm, tm, width, col=0):
    if rm == 1:
        return pl.BlockSpec((1, width), lambda i, *_: (0, col))
    return pl.BlockSpec((tm, width), lambda i, *_: (i, col))


def _norm_mod(x, g, mod, sc_col, sh_col, tm, out_dtype):
    m, d = x.shape
    rm = mod.shape[0]
    return pl.pallas_call(
        _norm_mod_kernel,
        grid=(m // tm,),
        in_specs=[pl.BlockSpec((tm, d), lambda i: (i, 0)),
                  pl.BlockSpec((1, d), lambda i: (0, 0)),
                  _row_spec(rm, tm, d, sc_col),
                  _row_spec(rm, tm, d, sh_col)],
        out_specs=pl.BlockSpec((tm, d), lambda i: (i, 0)),
        out_shape=jax.ShapeDtypeStruct((m, d), out_dtype),
        compiler_params=_cparams(("arbitrary",)),
        name="norm_mod",
    )(x, g.reshape(1, d), mod, mod)


def _final_norm_kernel(x_ref, g_ref, o_ref):
    x = x_ref[...]
    o_ref[...] = x * lax.rsqrt(jnp.mean(x * x, axis=-1, keepdims=True) + EPS) * g_ref[...]


def _final_norm(x, g, tm):
    m, d = x.shape
    return pl.pallas_call(
        _final_norm_kernel,
        grid=(m // tm,),
        in_specs=[pl.BlockSpec((tm, d), lambda i: (i, 0)), pl.BlockSpec((1, d), lambda i: (0, 0))],
        out_specs=pl.BlockSpec((tm, d), lambda i: (i, 0)),
        out_shape=jax.ShapeDtypeStruct((m, d), F32),
        compiler_params=_cparams(("arbitrary",)),
        name="final_norm",
    )(x, g.reshape(1, d))


def _rope128(x, cos, sin):
    outs = []
    for h in range(x.shape[1] // LANES):
        xs = x[:, h * LANES:(h + 1) * LANES]
        outs.append(xs * cos + pltpu.roll(xs, LANES // 2, 1) * sin)
    return outs[0] if len(outs) == 1 else jnp.concatenate(outs, axis=1)


def _rope64(x, cos, sin_a, sin_b):
    outs = []
    for h in range(x.shape[1] // LANES):
        xs = x[:, h * LANES:(h + 1) * LANES]
        outs.append(xs * cos + pltpu.roll(xs, 96, 1) * sin_a + pltpu.roll(xs, 32, 1) * sin_b)
    return outs[0] if len(outs) == 1 else jnp.concatenate(outs, axis=1)


def _proj_kernel(*refs, epi, scale):
    x_ref, w_ref = refs[0], refs[1]
    o_ref = refs[-1]
    acc = _dot(x_ref[...], w_ref[...])
    if epi == "rope128":
        acc = _rope128(acc, refs[2][...], refs[3][...])
    elif epi == "rope64":
        acc = _rope64(acc, refs[2][...], refs[3][...], refs[4][...])
    elif epi == "small":
        rot = _rope64(acc, refs[2][...], refs[3][...], refs[4][...])
        lane = lax.broadcasted_iota(I32, acc.shape, 1)
        acc = jnp.where(lane >= SM_IK, rot, acc)
    elif epi == "sigmoid":
        acc = _sigmoid(acc)
    if scale != 1.0:
        acc = acc * scale
    o_ref[...] = acc.astype(o_ref.dtype)


def _proj(x, w, tm, tn, out_dtype, epi=None, tables=(), scale=1.0):
    m, k = x.shape
    n = w.shape[1]
    in_specs = [pl.BlockSpec((tm, k), lambda i, j: (i, 0)),
                pl.BlockSpec((k, tn), lambda i, j: (0, j))]
    in_specs += [pl.BlockSpec((tm, LANES), lambda i, j: (i, 0)) for _ in tables]
    return pl.pallas_call(
        functools.partial(_proj_kernel, epi=epi, scale=scale),
        grid=(m // tm, n // tn),
        in_specs=in_specs,
        out_specs=pl.BlockSpec((tm, tn), lambda i, j: (i, j)),
        out_shape=jax.ShapeDtypeStruct((m, n), out_dtype),
        compiler_params=_cparams(("arbitrary", "arbitrary")),
        name="proj_" + (epi or "plain"),
    )(x, w, *tables)


def _cumsum_rows(a):
    q = a.shape[0]
    row = lax.broadcasted_iota(I32, a.shape, 0)
    k = 1
    while k < q:
        a = a + jnp.where(row >= k, pltpu.roll(a, k, 0), 0.0)
        k *= 2
    return a


def _expand_heads(v, e_ref):
    hi = v.astype(BF16)
    r1 = v - hi.astype(F32)
    mid = r1.astype(BF16)
    lo = (r1 - mid.astype(F32)).astype(BF16)
    e = e_ref[...]
    return (_dot(hi, e) + _dot(mid, e)) + _dot(lo, e)


def _pad_rows(x, rows):
    if x.shape[0] >= rows:
        return x
    return jnp.concatenate([x, jnp.zeros((rows - x.shape[0], x.shape[1]), x.dtype)], axis=0)


def _ssd_kernel(xbc_ref, z_ref, sm_ref, hist_ref, st0_ref, convw_ref, convb_ref, dtb_ref, negA_ref,
                dskip_ref, norm_ref, e_ref, y_ref, st_ref, xe_scr, st_scr, *, q):
    c = pl.program_id(1)
    width = SSD_GROUPS * 8 * SSD_HEAD_DIM
    gn = SSD_GROUPS * SSD_STATE
    hp = 8 * SSD_HEAD_DIM

    @pl.when(c == 0)
    def _():
        xe_scr[0:8, :] = hist_ref[...]
        st_scr[...] = st0_ref[...]

    xe_scr[8:8 + q, :] = xbc_ref[...]
    conv = convb_ref[...]
    for j in range(SSD_CONV):
        conv = conv + xe_scr[pl.ds(8 - (SSD_CONV - 1) + j, q), :] * convw_ref[j:j + 1, :]
    tail = xe_scr[q:q + 8, :]
    xe_scr[0:8, :] = tail
    act = _silu(conv)
    xs = act[:, :width]

    lane = lax.broadcasted_iota(I32, (q, LANES), 1)
    dt_in = sm_ref[...] + dtb_ref[...]
    dt = jnp.where(lane < 32, jnp.maximum(dt_in, 0.0) + jnp.log1p(jnp.exp(-jnp.abs(dt_in))), 0.0)
    a = dt * negA_ref[...]
    cum = _cumsum_rows(a)
    last = cum[q - 1:q, :]
    cum_t = _pad_rows(cum, LANES).T
    dt_t = _pad_rows(dt, LANES).T

    xs_b = xs.astype(BF16)
    decay_in = _expand_heads(jnp.exp(cum), e_ref)
    w_state = _expand_heads(jnp.exp(last - cum) * dt, e_ref)
    xw_b = (w_state * xs).astype(BF16)
    st_decay = decay_in[q - 1:q, :]

    row_i = lax.broadcasted_iota(I32, (q, q), 0)
    col_i = lax.broadcasted_iota(I32, (q, q), 1)
    causal = row_i >= col_i
    lane_lo = lax.broadcasted_iota(I32, (q, LANES), 1) < SSD_HEAD_DIM

    ys = []
    for g in range(SSD_GROUPS):
        b_g = act[:, width + g * SSD_STATE: width + (g + 1) * SSD_STATE].astype(BF16)
        c_g = act[:, width + gn + g * SSD_STATE: width + gn + (g + 1) * SSD_STATE].astype(BF16)
        cb = _dot_nt(c_g, b_g)
        st_g = st_scr[g]
        y_inter = _dot(c_g, st_g.astype(BF16)) * decay_in[:, g * hp:(g + 1) * hp]
        for pair in range(4):
            ws = []
            for hh in range(2):
                h = g * 8 + pair * 2 + hh
                seg = cum[:, h:h + 1] - cum_t[h:h + 1, :q]
                decay = jnp.exp(jnp.where(causal, seg, NEG_BIG))
                ws.append(((cb * decay) * dt_t[h:h + 1, :q]).astype(BF16))
            w2 = jnp.concatenate(ws, axis=1)
            col0 = g * hp + pair * LANES
            slab = xs_b[:, col0:col0 + LANES]
            zero = jnp.zeros_like(slab)
            rhs = jnp.concatenate([jnp.where(lane_lo, slab, zero), jnp.where(lane_lo, zero, slab)], axis=0)
            ys.append(_dot(w2, rhs) + y_inter[:, pair * LANES:(pair + 1) * LANES])
        b_t = _pad_rows(b_g.astype(F32), LANES).T.astype(BF16)
        upd = _dot(b_t, _pad_rows(xw_b[:, g * hp:(g + 1) * hp], LANES))
        st_scr[g] = st_g * st_decay[:, g * hp:(g + 1) * hp] + upd

    y = jnp.concatenate(ys, axis=1) + dskip_ref[...] * xs
    y = y * _silu(z_ref[...])
    y = y * lax.rsqrt(jnp.mean(y * y, axis=-1, keepdims=True) + EPS) * norm_ref[...]
    y_ref[...] = y.astype(y_ref.dtype)

    @pl.when(c == pl.num_programs(1) - 1)
    def _():
        st_ref[...] = st_scr[...]


def _ssd(xbc, z, small, hist8, st0, convw, convb, dtb, neg_a, dskip_e, norm, e_mat, q):
    b, l, xw = xbc.shape
    width = z.shape[2]
    nc = l // q
    seq = lambda bb, cc: (bb, cc, 0)
    fixed2 = lambda bb, cc: (0, 0)
    return pl.pallas_call(
        functools.partial(_ssd_kernel, q=q),
        grid=(b, nc),
        in_specs=[pl.BlockSpec((None, q, xw), seq),
                  pl.BlockSpec((None, q, width), seq),
                  pl.BlockSpec((None, q, LANES), seq),
                  pl.BlockSpec((None, 8, xw), lambda bb, cc: (bb, 0, 0)),
                  pl.BlockSpec((None, SSD_GROUPS, SSD_STATE, width // SSD_GROUPS), lambda bb, cc: (bb, 0, 0, 0)),
                  pl.BlockSpec((SSD_CONV, xw), fixed2),
                  pl.BlockSpec((1, xw), fixed2),
                  pl.BlockSpec((1, LANES), fixed2),
                  pl.BlockSpec((1, LANES), fixed2),
                  pl.BlockSpec((1, width), fixed2),
                  pl.BlockSpec((1, width), fixed2),
                  pl.BlockSpec((LANES, width), fixed2)],
        out_specs=[pl.BlockSpec((None, q, width), seq),
                   pl.BlockSpec((None, SSD_GROUPS, SSD_STATE, width // SSD_GROUPS), lambda bb, cc: (bb, 0, 0, 0))],
        out_shape=[jax.ShapeDtypeStruct((b, l, width), BF16),
                   jax.ShapeDtypeStruct((b, SSD_GROUPS, SSD_STATE, width // SSD_GROUPS), F32)],
        scratch_shapes=[pltpu.VMEM((q + 8, xw), F32),
                        pltpu.VMEM((SSD_GROUPS, SSD_STATE, width // SSD_GROUPS), F32)],
        compiler_params=_cparams(("arbitrary", "arbitrary")),
        name="ssd",
    )(xbc, z, small, hist8, st0, convw, convb, dtb, neg_a, dskip_e, norm, e_mat)


INT_MIN = -2 ** 31
NEG_INF_KEY = (0xFF800000 ^ 0x7FFFFFFF) - 2 ** 32


def _float_key(s):
    bits = pltpu.bitcast(jnp.where(s == 0.0, 0.0, s), I32)
    return jnp.where(bits < 0, bits ^ 0x7FFFFFFF, bits)


def _dsa_kernel(q_ref, iq_ref, sm_ref, k_ref, v_ref, ik2_ref, o_ref, key_scr, lg_scr, *, tq, tk, s_valid, pos0,
                topk):
    i = pl.program_id(1)
    row = lax.broadcasted_iota(I32, (tq, 1), 0)
    qpos = pos0 + i * tq + row
    limit = jnp.minimum((qpos // CHUNK + 1) * CHUNK, s_valid)
    last_pos = pos0 + i * tq + tq - 1
    n_keys = jnp.minimum((last_pos // CHUNK + 1) * CHUNK, s_valid)
    nkb = (n_keys + tk - 1) // tk

    w8 = (sm_ref[:, SM_IW:SM_IW + IDX_HEADS] * (IDX_HEADS ** -0.5)).astype(BF16).astype(F32)
    iq = iq_ref[...]
    col = lax.broadcasted_iota(I32, (1, tk), 1)

    def score_body(kb, carry):
        rhs = ik2_ref[kb]
        acc = jnp.zeros((tq, tk), F32)
        for p in range(IDX_HEADS // 2):
            x = _dot_nt(iq[:, p * LANES:(p + 1) * LANES], rhs) * (IDX_DIM ** -0.5)
            rel = jnp.maximum(x, 0.0).astype(BF16).astype(F32)
            acc = acc + rel[:, :tk] * w8[:, 2 * p:2 * p + 1]
            acc = acc + rel[:, tk:] * w8[:, 2 * p + 1:2 * p + 2]
        s = jnp.where(kb * tk + col < limit, acc, -jnp.inf)
        key_scr[:, pl.ds(pl.multiple_of(kb * tk, tk), tk)] = _float_key(s)
        return carry

    lax.fori_loop(0, nkb, score_body, 0)

    def count_ge(cand):
        def body(kb, acc):
            blk = key_scr[:, pl.ds(pl.multiple_of(kb * tk, tk), tk)]
            m = jnp.where(blk >= cand, 1, 0)
            for c in range(tk // LANES):
                acc = acc + m[:, c * LANES:(c + 1) * LANES]
            return acc
        acc = lax.fori_loop(0, nkb, body, jnp.zeros((tq, LANES), I32))
        return jnp.sum(acc, axis=1, keepdims=True)

    base = jnp.where(count_ge(jnp.zeros((tq, 1), I32)) >= topk, 0, INT_MIN)

    def bit_body(b, base):
        cand = base + jnp.left_shift(jnp.int32(1), 30 - b)
        return jnp.where(count_ge(cand) >= topk, cand, base)

    thr = lax.fori_loop(0, 31, bit_body, base)

    finite_thr = thr != NEG_INF_KEY
    n_ge = count_ge(thr)
    need = topk - count_ge(thr + 1)
    tie_rows = jnp.logical_and(finite_thr, n_ge > topk)

    @pl.when(jnp.max(jnp.where(tie_rows, 1, 0)) > 0)
    def _():
        def count_eq_below(cut):
            def body(kb, acc):
                blk = key_scr[:, pl.ds(pl.multiple_of(kb * tk, tk), tk)]
                m = jnp.where(blk == thr, jnp.where(kb * tk + col < cut, 1, 0), 0)
                for c in range(tk // LANES):
                    acc = acc + m[:, c * LANES:(c + 1) * LANES]
                return acc
            acc = lax.fori_loop(0, nkb, body, jnp.zeros((tq, LANES), I32))
            return jnp.sum(acc, axis=1, keepdims=True)

        def cut_body(b, c0):
            cand = c0 + jnp.left_shift(jnp.int32(1), 15 - b)
            return jnp.where(count_eq_below(cand) < need, cand, c0)

        c0 = lax.fori_loop(0, 16, cut_body, jnp.zeros((tq, 1), I32))
        cut = jnp.where(tie_rows, c0 + 1, jnp.int32(2 ** 30))

        def fix_body(kb, carry):
            sl = pl.ds(pl.multiple_of(kb * tk, tk), tk)
            blk = key_scr[:, sl]
            drop = jnp.where(blk == thr, jnp.where(kb * tk + col >= cut, 1, 0), 0)
            key_scr[:, sl] = blk - drop
            return carry

        lax.fori_loop(0, nkb, fix_body, 0)

    thr_eff = jnp.where(finite_thr, thr, thr + 1)

    grp = ATT_HEADS // ATT_KV_HEADS
    scale = ATT_HEAD_DIM ** -0.5
    for kh in range(ATT_KV_HEADS):
        qs = jnp.concatenate([q_ref[:, (kh * grp + g) * LANES:(kh * grp + g + 1) * LANES] for g in range(grp)],
                             axis=0)

        def logits_body(kb, m_acc, kh=kh, qs=qs):
            sl = pl.ds(pl.multiple_of(kb * tk, tk), tk)
            kblk = k_ref[sl, kh * LANES:(kh + 1) * LANES]
            bias = jnp.where(key_scr[:, sl] >= thr_eff, 0.0, NEG_BIG)
            s = _dot_nt(qs, kblk) * scale + jnp.concatenate([bias] * grp, axis=0)
            lg_scr[:, sl] = s
            for c in range(tk // LANES):
                m_acc = jnp.maximum(m_acc, s[:, c * LANES:(c + 1) * LANES])
            return m_acc

        m_acc = lax.fori_loop(0, nkb, logits_body, jnp.full((grp * tq, LANES), NEG_BIG, F32))
        m = jnp.max(m_acc, axis=1, keepdims=True)

        def exp_body(kb, l_acc):
            sl = pl.ds(pl.multiple_of(kb * tk, tk), tk)
            e = jnp.exp(lg_scr[:, sl] - m)
            lg_scr[:, sl] = e
            for c in range(tk // LANES):
                l_acc = l_acc + e[:, c * LANES:(c + 1) * LANES]
            return l_acc

        l_acc = lax.fori_loop(0, nkb, exp_body, jnp.zeros((grp * tq, LANES), F32))
        inv = 1.0 / jnp.sum(l_acc, axis=1, keepdims=True)

        def pv_body(kb, acc, kh=kh):
            sl = pl.ds(pl.multiple_of(kb * tk, tk), tk)
            p = (lg_scr[:, sl] * inv).astype(BF16)
            return acc + _dot(p, v_ref[sl, kh * LANES:(kh + 1) * LANES])

        out = lax.fori_loop(0, nkb, pv_body, jnp.zeros((grp * tq, LANES), F32))
        for g in range(grp):
            o_ref[:, (kh * grp + g) * LANES:(kh * grp + g + 1) * LANES] = out[g * tq:(g + 1) * tq].astype(o_ref.dtype)


def _dsa(q, iq, small, k, v, ik2, tq, tk, s_valid, pos0):
    b, t, aw = q.shape
    s_pad = k.shape[1]
    topk = min(TOPK_MAX, s_valid // 4)
    blk = lambda bb, ii: (bb, ii, 0)
    whole = lambda bb, ii: (bb, 0, 0)
    return pl.pallas_call(
        functools.partial(_dsa_kernel, tq=tq, tk=tk, s_valid=s_valid, pos0=pos0, topk=topk),
        grid=(b, t // tq),
        in_specs=[pl.BlockSpec((None, tq, aw), blk),
                  pl.BlockSpec((None, tq, iq.shape[2]), blk),
                  pl.BlockSpec((None, tq, LANES), blk),
                  pl.BlockSpec((None, s_pad, k.shape[2]), whole, pipeline_mode=pl.Buffered(1)),
                  pl.BlockSpec((None, s_pad, v.shape[2]), whole, pipeline_mode=pl.Buffered(1)),
                  pl.BlockSpec((None, s_pad // tk, 2 * tk, LANES), lambda bb, ii: (bb, 0, 0, 0),
                               pipeline_mode=pl.Buffered(1))],
        out_specs=pl.BlockSpec((None, tq, aw), blk),
        out_shape=jax.ShapeDtypeStruct((b, t, aw), BF16),
        scratch_shapes=[pltpu.VMEM((tq, s_pad), I32),
                        pltpu.VMEM((ATT_HEADS // ATT_KV_HEADS * tq, s_pad), F32)],
        compiler_params=_cparams(("arbitrary", "arbitrary")),
        name="dsa",
    )(q, iq, small, k, v, ik2)


def _pool_kernel(u_ref, hist_ref, lin_ref, scale_ref, o_ref, ext_scr, *, tp, pos0):
    c = pl.program_id(1)
    gd = u_ref.shape[1] // len(POOL_WINDOWS)

    @pl.when(c == 0)
    def _():
        ext_scr[0:16, :] = hist_ref[...]

    u = u_ref[...]
    ext_scr[16:16 + tp, :] = u
    pos = pos0 + c * tp + lax.broadcasted_iota(I32, (tp, 1), 0)
    for gi, w in enumerate(POOL_WINDOWS):
        lo = gi * gd
        acc = u[:, lo:lo + gd]
        for j in range(1, w):
            acc = acc + ext_scr[pl.ds(16 - j, tp), lo:lo + gd]
        cnt = jnp.minimum(pos + 1, w).astype(F32)
        pooled = acc / cnt - u[:, lo:lo + gd]
        y = _dot(pooled.astype(BF16), lin_ref[gi])
        o_ref[:, lo:lo + gd] = (y * scale_ref[:, lo:lo + gd]).astype(o_ref.dtype)
    tail = ext_scr[tp:tp + 16, :]
    ext_scr[0:16, :] = tail


def _pool(u, hist16, lin, scale, tp, pos0):
    b, t, c = u.shape
    g = len(POOL_WINDOWS)
    return pl.pallas_call(
        functools.partial(_pool_kernel, tp=tp, pos0=pos0),
        grid=(b, t // tp),
        in_specs=[pl.BlockSpec((None, tp, c), lambda bb, cc: (bb, cc, 0)),
                  pl.BlockSpec((None, 16, c), lambda bb, cc: (bb, 0, 0)),
                  pl.BlockSpec((g, c // g, c // g), lambda bb, cc: (0, 0, 0)),
                  pl.BlockSpec((1, c), lambda bb, cc: (0, 0))],
        out_specs=pl.BlockSpec((None, tp, c), lambda bb, cc: (bb, cc, 0)),
        out_shape=jax.ShapeDtypeStruct((b, t, c), BF16),
        scratch_shapes=[pltpu.VMEM((tp + 16, c), F32)],
        compiler_params=_cparams(("arbitrary", "arbitrary")),
        name="pool",
    )(u, hist16, lin, scale)


def _merge_kernel(ya_ref, yb_ref, yc_ref, wa_ref, wb_ref, wc_ref, ga_ref, gb_ref, gc_ref, o_ref):
    m = ga_ref[...].astype(F32) * _dot(ya_ref[...], wa_ref[...])
    m = m + gb_ref[...].astype(F32) * _dot(yb_ref[...], wb_ref[...])
    m = m + gc_ref[...].astype(F32) * _dot(yc_ref[...], wc_ref[...])
    o_ref[...] = m.astype(o_ref.dtype)


def _merge(ya, yb, yc, wa, wb, wc, gates, tm, tn):
    m, d = ya.shape[0], wa.shape[1]
    nj = d // tn
    xspec = lambda a: pl.BlockSpec((tm, a.shape[1]), lambda i, j: (i, 0))
    wspec = lambda a: pl.BlockSpec((a.shape[0], tn), lambda i, j: (0, j))
    gspec = lambda br: pl.BlockSpec((tm, tn), lambda i, j, br=br: (i, br * nj + j))
    return pl.pallas_call(
        _merge_kernel,
        grid=(m // tm, nj),
        in_specs=[xspec(ya), xspec(yb), xspec(yc), wspec(wa), wspec(wb), wspec(wc), gspec(0), gspec(1), gspec(2)],
        out_specs=pl.BlockSpec((tm, tn), lambda i, j: (i, j)),
        out_shape=jax.ShapeDtypeStruct((m, d), BF16),
        compiler_params=_cparams(("arbitrary", "arbitrary")),
        name="merge",
    )(ya, yb, yc, wa, wb, wc, gates, gates, gates)


def _outproj_kernel(m_ref, w_ref, x_ref, g_ref, o_ref):
    o_ref[...] = x_ref[...] + g_ref[...] * _dot(m_ref[...], w_ref[...])


def _outproj(merged, w, x, mod, g_col, tm, tn):
    m, d = x.shape
    rm = mod.shape[0]
    nj = d // tn
    if rm == 1:
        gspec = pl.BlockSpec((1, tn), lambda i, j: (0, g_col * nj + j))
    else:
        gspec = pl.BlockSpec((tm, tn), lambda i, j: (i, g_col * nj + j))
    return pl.pallas_call(
        _outproj_kernel,
        grid=(m // tm, nj),
        in_specs=[pl.BlockSpec((tm, merged.shape[1]), lambda i, j: (i, 0)),
                  pl.BlockSpec((w.shape[0], tn), lambda i, j: (0, j)),
                  pl.BlockSpec((tm, tn), lambda i, j: (i, j)),
                  gspec],
        out_specs=pl.BlockSpec((tm, tn), lambda i, j: (i, j)),
        out_shape=jax.ShapeDtypeStruct((m, d), F32),
        compiler_params=_cparams(("arbitrary", "arbitrary")),
        name="outproj",
    )(merged, w, x, mod)


def _router_kernel(h_ref, w_ref, b_ref, o_ref):
    logits = _dot(h_ref[...], w_ref[...]) + b_ref[...]
    lane = lax.broadcasted_iota(I32, logits.shape, 1)
    lg = jnp.where(lane < N_EXPERT_GROUPS, logits, -jnp.inf)
    gmax = jnp.max(lg, axis=1, keepdims=True)
    grp = jnp.min(jnp.where(lg == gmax, lane, LANES), axis=1, keepdims=True)
    g_w = 1.0 / jnp.sum(jnp.exp(lg - gmax), axis=1, keepdims=True)
    first = N_EXPERT_GROUPS + grp * EXPERTS_PER_GROUP
    le = jnp.where(jnp.logical_and(lane >= first, lane < first + EXPERTS_PER_GROUP), logits, -jnp.inf)
    top1 = jnp.max(le, axis=1, keepdims=True)
    i1 = jnp.min(jnp.where(le == top1, lane, LANES), axis=1, keepdims=True)
    le2 = jnp.where(lane == i1, -jnp.inf, le)
    top2 = jnp.max(le2, axis=1, keepdims=True)
    i2 = jnp.min(jnp.where(le2 == top2, lane, LANES), axis=1, keepdims=True)
    e21 = jnp.exp(top2 - top1)
    p1 = 1.0 / (1.0 + e21)
    p2 = e21 * p1
    out = jnp.where(lane == 0, (i1 - N_EXPERT_GROUPS).astype(F32),
          jnp.where(lane == 1, (i2 - N_EXPERT_GROUPS).astype(F32),
          jnp.where(lane == 2, g_w * p1,
          jnp.where(lane == 3, g_w * p2, 0.0))))
    o_ref[...] = out


def _router(h2, w_r, b_r, tm):
    m, d = h2.shape
    return pl.pallas_call(
        _router_kernel,
        grid=(m // tm,),
        in_specs=[pl.BlockSpec((tm, d), lambda i: (i, 0)),
                  pl.BlockSpec((d, LANES), lambda i: (0, 0)),
                  pl.BlockSpec((1, LANES), lambda i: (0, 0))],
        out_specs=pl.BlockSpec((tm, LANES), lambda i: (i, 0)),
        out_shape=jax.ShapeDtypeStruct((m, LANES), F32),
        compiler_params=_cparams(("arbitrary",)),
        name="router",
    )(h2, w_r, b_r)


def _expert_kernel(blk_e_ref, nblk_ref, x_ref, wg_ref, wu_ref, wd_ref, o_ref):
    b = pl.program_id(0)

    @pl.when(b < nblk_ref[0])
    def _():
        x = x_ref[...]
        g = _dot(x, wg_ref[0])
        u = _dot(x, wu_ref[0])
        o_ref[...] = _dot((_silu(g) * u).astype(BF16), wd_ref[0])

    @pl.when(b >= nblk_ref[0])
    def _():
        o_ref[...] = jnp.zeros_like(o_ref)


def _experts(buf, blk_e, nblk, wg, wu, wd, bs):
    rows, d = buf.shape
    ff = wg.shape[2]
    grid_spec = pltpu.PrefetchScalarGridSpec(
        num_scalar_prefetch=2,
        grid=(rows // bs,),
        in_specs=[pl.BlockSpec((bs, d), lambda b, be, nb: (b, 0)),
                  pl.BlockSpec((1, d, ff), lambda b, be, nb: (be[b], 0, 0)),
                  pl.BlockSpec((1, d, ff), lambda b, be, nb: (be[b], 0, 0)),
                  pl.BlockSpec((1, ff, d), lambda b, be, nb: (be[b], 0, 0))],
        out_specs=pl.BlockSpec((bs, d), lambda b, be, nb: (b, 0)),
    )
    return pl.pallas_call(
        _expert_kernel,
        grid_spec=grid_spec,
        out_shape=jax.ShapeDtypeStruct((rows, d), F32),
        compiler_params=_cparams(("arbitrary",)),
        name="experts",
    )(blk_e, nblk, buf, wg, wu, wd)


def _combine_kernel(x_ref, o0_ref, o1_ref, r_ref, g_ref, o_ref):
    r = r_ref[...]
    moe = o0_ref[...] * r[:, 2:3] + o1_ref[...] * r[:, 3:4]
    o_ref[...] = x_ref[...] + g_ref[...] * moe


def _combine(x, o0, o1, route, mod, g_col, tm):
    m, d = x.shape
    rm = mod.shape[0]
    row = pl.BlockSpec((tm, d), lambda i: (i, 0))
    return pl.pallas_call(
        _combine_kernel,
        grid=(m // tm,),
        in_specs=[row, row, row, pl.BlockSpec((tm, LANES), lambda i: (i, 0)), _row_spec(rm, tm, d, g_col)],
        out_specs=row,
        out_shape=jax.ShapeDtypeStruct((m, d), F32),
        compiler_params=_cparams(("arbitrary",)),
        name="combine",
    )(x, o0, o1, route, mod)


def _moe_dispatch(route, bs):
    n = route.shape[0]
    eid = route[:, 0:2].astype(I32).reshape(-1)
    m = eid.shape[0]
    order = jnp.argsort(eid)
    se = eid[order]
    sizes = jnp.bincount(eid, length=N_EXPERTS).astype(I32)
    padded = (sizes + bs - 1) // bs * bs
    pad_end = jnp.cumsum(padded)
    pad_start = pad_end - padded
    start = jnp.cumsum(sizes) - sizes
    dest = pad_start[se] + jnp.arange(m, dtype=I32) - start[se]
    n_blocks = -(-m // bs) + N_EXPERTS
    tok = order // 2
    src = jnp.full((n_blocks * bs,), n, I32).at[dest].set(tok.astype(I32))
    blk_e = jnp.minimum(jnp.searchsorted(pad_end, jnp.arange(n_blocks, dtype=I32) * bs, side='right'),
                        N_EXPERTS - 1).astype(I32)
    nblk = (pad_end[-1] // bs).astype(I32).reshape(1)
    last_e = blk_e[jnp.maximum(nblk[0] - 1, 0)]
    blk_e = jnp.where(jnp.arange(n_blocks) < nblk[0], blk_e, last_e)
    slot_of = jnp.zeros((m,), I32).at[order].set(dest)
    return src, blk_e, nblk, slot_of.reshape(n, 2)


def _rope_tables(pos):
    posf = pos.astype(F32)[:, None]
    half = ATT_HEAD_DIM // 2
    inv = ROPE_THETA ** (-jnp.arange(half, dtype=F32) / half)
    ang = posf * inv[None, :]
    c, s = jnp.cos(ang), jnp.sin(ang)
    cos128 = jnp.concatenate([c, c], axis=1)
    sin128 = jnp.concatenate([-s, s], axis=1)
    half = IDX_DIM // 2
    inv = ROPE_THETA ** (-jnp.arange(half, dtype=F32) / half)
    ang = posf * inv[None, :]
    c, s = jnp.cos(ang), jnp.sin(ang)
    z = jnp.zeros_like(s)
    cos64 = jnp.concatenate([c, c, c, c], axis=1)
    sin_a = jnp.concatenate([-s, z, -s, z], axis=1)
    sin_b = jnp.concatenate([z, s, z, s], axis=1)
    return (cos128, sin128), (cos64, sin_a, sin_b)


def _split_w_in(w_in_l, d):
    sizes = (d, d + 2 * SSD_GROUPS * SSD_STATE, d // SSD_HEAD_DIM, ATT_HEADS * ATT_HEAD_DIM,
             ATT_KV_HEADS * ATT_HEAD_DIM, ATT_KV_HEADS * ATT_HEAD_DIM, IDX_HEADS * IDX_DIM, IDX_DIM, IDX_HEADS,
             d // 2, 3 * d)
    parts, acc = [], 0
    for s in sizes:
        parts.append(w_in_l[:, acc:acc + s])
        acc += s
    wz, wxbc, wdt, wq, wk, wv, wiq, wik, wiw, wu, wgate = parts
    small = jnp.concatenate([wdt, wiw, jnp.zeros((w_in_l.shape[0], SM_IK - SM_IW - IDX_HEADS), w_in_l.dtype), wik],
                            axis=1)
    cast = lambda a: a.astype(BF16)
    return dict(z=cast(wz), xbc=cast(wxbc), q=cast(wq), k=cast(wk), v=cast(wv), iq=cast(wiq), u=cast(wu),
                gate=cast(wgate), small=cast(small))


def _lane_pad(v, fill=0.0):
    return jnp.concatenate([v.astype(F32), jnp.full((LANES - v.shape[0],), fill, F32)]).reshape(1, LANES)


def _ik_blockdiag(ik, tk):
    b, s, dd = ik.shape
    blk = ik.reshape(b, s // tk, tk, dd)
    z = jnp.zeros_like(blk)
    return jnp.concatenate([jnp.concatenate([blk, z], axis=3), jnp.concatenate([z, blk], axis=3)], axis=2)


def _group_layer(x, mod, lw, pos, pos0, hist_conv, st0, k_past, v_past, ik_past, hist_pool, cfg):
    b, t, d = x.shape
    m = b * t
    tm, tn, q_ssd, tq, tk, tp = cfg
    rope_att, rope_idx = _rope_tables(pos)
    tile_b = lambda tab: jnp.tile(tab, (b, 1))
    rope_att = tuple(tile_b(a) for a in rope_att)
    rope_idx = tuple(tile_b(a) for a in rope_idx)

    x2 = x.reshape(m, d)
    h = _norm_mod(x2, lw['norm_mix'], mod, 1, 0, tm, BF16)
    w = lw['w_in']
    z = _proj(h, w['z'], tm, tn, F32)
    xbc = _proj(h, w['xbc'], tm, tn, F32)
    qh = _proj(h, w['q'], tm, tn, BF16, 'rope128', rope_att)
    k_new = _proj(h, w['k'], tm, 256, F32, 'rope128', rope_att)
    v_new = _proj(h, w['v'], tm, 256, F32)
    iqh = _proj(h, w['iq'], tm, tn, BF16, 'rope64', rope_idx)
    small = _proj(h, w['small'], tm, LANES, F32, 'small', rope_idx)
    u = _proj(h, w['u'], tm, tn, F32)
    gates = _proj(h, w['gate'], tm, tn, F32, 'sigmoid')

    ya, st_new = _ssd(xbc.reshape(b, t, -1), z.reshape(b, t, -1), small.reshape(b, t, LANES), hist_conv, st0,
                      lw['conv_w'], lw['conv_b'], lw['dt_bias'], lw['neg_a'], lw['d_skip_e'], lw['ssd_norm'],
                      lw['e_mat'], q_ssd)
    full_conv = jnp.concatenate([hist_conv[:, 8 - (SSD_CONV - 1):], xbc.reshape(b, t, -1)], axis=1)
    conv_new = full_conv[:, -(SSD_CONV - 1):]

    ik_new = small.reshape(b, t, LANES)[:, :, SM_IK:]
    k3, v3 = k_new.reshape(b, t, -1), v_new.reshape(b, t, -1)
    k_all = jnp.concatenate([k_past, k3], axis=1)
    v_all = jnp.concatenate([v_past, v3], axis=1)
    ik_all = jnp.concatenate([ik_past, ik_new], axis=1)
    s_valid = k_all.shape[1]
    s_pad = -(-s_valid // tk) * tk
    padk = lambda a: jnp.pad(a, ((0, 0), (0, s_pad - s_valid), (0, 0))).astype(BF16)
    yb = _dsa(qh.reshape(b, t, -1), iqh.reshape(b, t, -1), small.reshape(b, t, LANES),
              padk(k_all), padk(v_all), _ik_blockdiag(padk(ik_all), tk), tq, tk, s_valid, pos0)

    u3 = u.reshape(b, t, -1)
    yc = _pool(u3, hist_pool, lw['pool_lin'], lw['pool_scale'], tp, pos0)
    pool_new = jnp.concatenate([hist_pool[:, 1:], u3], axis=1)[:, -POOL_HIST:]

    merged = _merge(ya.reshape(m, -1), yb.reshape(m, -1), yc.reshape(m, -1),
                    lw['w_br_a'], lw['w_br_b'], lw['w_br_c'], gates, min(tm, 512), tn)
    x2 = _outproj(merged, lw['w_out'], x2, mod, 2, tm, tn)
    h2 = _norm_mod(x2, lw['norm_ffn'], mod, 4, 3, tm, BF16)
    return x2, h2, (k_new, v_new, ik_new, conv_new, st_new, pool_new)


def kernel(x_prompt, x_sample, cache_k, cache_v, cache_idx_k, state_conv, state_ssm, state_pool, c_prompt, c_sample, w_mod, b_mod, norm_mix, w_in, conv_w, conv_b, dt_bias, a_log, d_skip, ssd_norm, w_br_a, w_br_b, w_br_c, w_out, pool_lin, pool_scale, norm_ffn, w_router_group, b_router_group, w_router_expert, b_router_expert, w_gate, w_up, w_down, final_norm):
    depth = w_mod.shape[0]
    bp, tp_, d = x_prompt.shape
    bs_, ts, _ = x_sample.shape
    past = cache_k.shape[2]
    heads = d // SSD_HEAD_DIM
    hg = heads // SSD_GROUPS
    kvw = ATT_KV_HEADS * ATT_HEAD_DIM
    np_rows, ns_rows = bp * tp_, bs_ * ts
    moe_bs = 256

    pos_p = jnp.arange(tp_, dtype=I32)
    pos_s = past + jnp.arange(ts, dtype=I32)

    rows = bp + bs_
    rpad = -(-rows // 16) * 16
    c_all = jnp.concatenate([c_prompt, c_sample, jnp.zeros((rpad - rows, d), F32)], axis=0)
    mod_all = _mod_all(c_all, w_mod, b_mod)

    e_mat = (jnp.arange(LANES)[:, None] == (jnp.arange(d)[None, :] // SSD_HEAD_DIM)).astype(BF16)

    def to_st(s):
        bsz = s.shape[0]
        return s.reshape(bsz, SSD_GROUPS, hg, SSD_HEAD_DIM, SSD_STATE).transpose(0, 1, 4, 2, 3).reshape(
            bsz, SSD_GROUPS, SSD_STATE, hg * SSD_HEAD_DIM)

    def from_st(s):
        bsz = s.shape[0]
        return s.reshape(bsz, SSD_GROUPS, SSD_STATE, hg, SSD_HEAD_DIM).transpose(0, 1, 3, 4, 2).reshape(
            bsz, heads, SSD_HEAD_DIM, SSD_STATE)

    tm_p = min(1024, np_rows)
    cfg_p = (tm_p, 512, min(CHUNK, tp_), min(128, tp_), min(512, tp_), min(512, tp_))
    cfg_s = (ns_rows, 512, ts, ts, 128, ts)
    n_tok = np_rows + ns_rows
    tm_r = next(t for t in (640, 512, 256, 128, 64, 32, 16, 8) if n_tok % t == 0)

    xp, xs = x_prompt, x_sample
    st_p, st_s = [], []
    for l in range(depth):
        lw = dict(
            norm_mix=norm_mix[l], w_in=_split_w_in(w_in[l], d),
            conv_w=conv_w[l], conv_b=conv_b[l].reshape(1, -1),
            dt_bias=_lane_pad(dt_bias[l]), neg_a=_lane_pad(-jnp.exp(a_log[l].astype(F32))),
            d_skip_e=jnp.repeat(d_skip[l].astype(F32), SSD_HEAD_DIM).reshape(1, d),
            ssd_norm=ssd_norm[l].reshape(1, d), e_mat=e_mat,
            w_br_a=w_br_a[l].astype(BF16), w_br_b=w_br_b[l].astype(BF16), w_br_c=w_br_c[l].astype(BF16),
            w_out=w_out[l].astype(BF16), pool_lin=pool_lin[l].astype(BF16), pool_scale=pool_scale[l].reshape(1, -1),
            norm_ffn=norm_ffn[l],
        )
        mod_p = mod_all[l, 0:bp]
        mod_s = jnp.repeat(mod_all[l, bp:bp + bs_], ts, axis=0)

        zeros = lambda *s: jnp.zeros(s, F32)
        xp2, h2p, new_p = _group_layer(
            xp, mod_p, lw, pos_p, 0, zeros(bp, 8, conv_w.shape[2]), zeros(bp, SSD_GROUPS, SSD_STATE, hg * SSD_HEAD_DIM),
            zeros(bp, 0, kvw), zeros(bp, 0, kvw), zeros(bp, 0, IDX_DIM), zeros(bp, 16, d // 2), cfg_p)
        hist_conv_s = jnp.concatenate([zeros(bs_, 8 - (SSD_CONV - 1), conv_w.shape[2]), state_conv[l]], axis=1)
        hist_pool_s = jnp.concatenate([zeros(bs_, 1, d // 2), state_pool[l]], axis=1)
        xs2, h2s, new_s = _group_layer(
            xs, mod_s, lw, pos_s, past, hist_conv_s, to_st(state_ssm[l]),
            cache_k[l].reshape(bs_, past, kvw), cache_v[l].reshape(bs_, past, kvw), cache_idx_k[l],
            hist_pool_s, cfg_s)

        h2 = jnp.concatenate([h2p, h2s], axis=0)
        n = h2.shape[0]
        w_r = jnp.concatenate([w_router_group[l], w_router_expert[l],
                               jnp.zeros((d, LANES - N_EXPERT_GROUPS - N_EXPERTS), F32)], axis=1).astype(BF16)
        b_r = _lane_pad(jnp.concatenate([b_router_group[l], b_router_expert[l]]))
        route = _router(h2, w_r, b_r, tm_r)
        src, blk_e, nblk, slot_of = _moe_dispatch(route, moe_bs)
        h2z = jnp.concatenate([h2, jnp.zeros((1, d), BF16)], axis=0)
        buf = h2z[src]
        eo = _experts(buf, blk_e, nblk, w_gate[l].astype(BF16), w_up[l].astype(BF16), w_down[l].astype(BF16), moe_bs)
        o0, o1 = eo[slot_of[:, 0]], eo[slot_of[:, 1]]
        xp = _combine(xp2, o0[:np_rows], o1[:np_rows], route[:np_rows], mod_p, 5, min(256, tm_p)).reshape(bp, tp_, d)
        xs = _combine(xs2, o0[np_rows:], o1[np_rows:], route[np_rows:], mod_s, 5, ns_rows).reshape(bs_, ts, d)

        def pack(new, bsz, t):
            k_new, v_new, ik_new, conv_new, st_new, pool_new = new
            return (k_new.reshape(bsz, t, ATT_KV_HEADS, ATT_HEAD_DIM), v_new.reshape(bsz, t, ATT_KV_HEADS, ATT_HEAD_DIM),
                    ik_new, conv_new, from_st(st_new), pool_new)
        st_p.append(pack(new_p, bp, tp_))
        st_s.append(pack(new_s, bs_, ts))

    y_prompt = _final_norm(xp.reshape(np_rows, d), final_norm, tm_p).reshape(bp, tp_, d)
    y_sample = _final_norm(xs.reshape(ns_rows, d), final_norm, ns_rows).reshape(bs_, ts, d)
    outs_p = [jnp.stack([s[i] for s in st_p]) for i in range(6)]
    outs_s = [jnp.stack([s[i] for s in st_s]) for i in range(6)]
    return (y_prompt, y_sample, *outs_p, *outs_s)
```
